```python
import jax, jax.numpy as jnp
from jax import lax
import numpy as np

D_MODEL = 2048
BATCH = 16
SEQ = 2048
DEPTH = 1
DEC_BATCH = 16
DEC_SEQ = 16
PAST_LEN = 2048

CHUNK = 64
H_A = 16
DK_A = 128
DV_A = 128
CONV_W = 4
GDN_QK = H_A * DK_A
GDN_V = H_A * DV_A
GDN_CONV_DIM = 2 * GDN_QK + GDN_V
H_B = 8
DQK_B = 128
DV_B = 256
M_QK = H_B * DQK_B
M_V = H_B * DV_B
PEER_HEADS = 8
N_KEYS = 128
N_EXPERTS = N_KEYS * N_KEYS
PEER_DKEY = 256
PEER_TOPK = 16
PEER_BLOCK = 128
PLE_DIM = 256
ALPHA = (2 * DEPTH) ** 0.25
BETA = (8 * DEPTH) ** -0.25
LN_EPS = 1e-5
RMS_EPS = 1e-6
IN_SPLITS = (GDN_CONV_DIM, GDN_V, H_A, H_A, M_QK, M_QK, M_V, M_V, H_B, H_B, D_MODEL, D_MODEL)
N_IN = sum(IN_SPLITS)

kernel_name = "hybrid_gdn_mlstm_peer_stream_step"


def _layer_norm(x, g, b):
    xf = x.astype(jnp.float32)
    mu = jnp.mean(xf, -1, keepdims=True)
    var = jnp.mean(jnp.square(xf - mu), -1, keepdims=True)
    return ((xf - mu) * lax.rsqrt(var + LN_EPS) * g + b).astype(x.dtype)


def _rms_norm(x, w):
    return x * lax.rsqrt(jnp.mean(x * x, -1, keepdims=True) + RMS_EPS) * w


def _l2norm(x):
    return x * lax.rsqrt(jnp.sum(x * x, -1, keepdims=True) + 1e-6)


def _chunk_len(L):
    return CHUNK if L % CHUNK == 0 else L


def _to_chunks(x, c):
    B, L, H = x.shape[:3]
    rest = x.shape[3:]
    x = x.reshape((B, L // c, c, H) + rest)
    return x.transpose((1, 0, 3, 2) + tuple(range(4, x.ndim)))


def _from_chunks(x):
    n, B, H, c, D = x.shape
    return x.transpose(1, 0, 3, 2, 4).reshape(B, n * c, H, D)


def _gated_delta_rule(q, k, v, g, beta, s0):
    c = _chunk_len(q.shape[1])
    q, k, v = _to_chunks(q, c), _to_chunks(k, c), _to_chunks(v, c)
    g, beta = _to_chunks(g, c), _to_chunks(beta, c)
    G = jnp.cumsum(g, -1)
    incl = jnp.tril(jnp.ones((c, c), bool))
    strict = jnp.tril(jnp.ones((c, c), bool), -1)
    decay = jnp.exp(jnp.where(incl, G[..., :, None] - G[..., None, :], -jnp.inf))
    kb = k * beta[..., None]
    a_mat = jnp.einsum('nbhid,nbhjd->nbhij', kb, k) * decay
    a_mat = jnp.where(strict, a_mat, 0.0) + jnp.eye(c, dtype=a_mat.dtype)
    rhs = jnp.concatenate([v * beta[..., None], kb * jnp.exp(G)[..., None]], -1)
    sol = lax.linalg.triangular_solve(a_mat, rhs, left_side=True, lower=True, unit_diagonal=True)
    dv = v.shape[-1]
    u_pre, w = sol[..., :dv], sol[..., dv:]
    attn = jnp.einsum('nbhid,nbhjd->nbhij', q, k) * decay
    q_g = q * jnp.exp(G)[..., None]
    k_dec = k * jnp.exp(G[..., -1:] - G)[..., None]
    g_end = jnp.exp(G[..., -1])

    def step(s, inp):
        u_pre_i, w_i, attn_i, q_i, k_i, ge = inp
        u = u_pre_i - jnp.einsum('bhck,bhkv->bhcv', w_i, s)
        o = jnp.einsum('bhck,bhkv->bhcv', q_i, s) + jnp.einsum('bhij,bhjv->bhiv', attn_i, u)
        s = s * ge[..., None, None] + jnp.einsum('bhck,bhcv->bhkv', k_i, u)
        return s, o

    s_end, o = lax.scan(step, s0, (u_pre, w, attn, q_g, k_dec, g_end))
    return _from_chunks(o), s_end


def _mlstm(q, k, v, ig, lf, c0, n0, m0):
    c = _chunk_len(q.shape[1])
    q, k, v = _to_chunks(q, c), _to_chunks(k, c), _to_chunks(v, c)
    ig, lf = _to_chunks(ig, c), _to_chunks(lf, c)
    F = jnp.cumsum(lf, -1)
    incl = jnp.tril(jnp.ones((c, c), bool))
    d_log = jnp.where(incl, F[..., :, None] - F[..., None, :] + ig[..., None, :], -jnp.inf)
    i_max = jnp.max(d_log, -1)
    w_intra = jnp.exp(d_log - i_max[..., None]) * jnp.einsum('nbhid,nbhjd->nbhij', q, k)
    num_intra = jnp.einsum('nbhij,nbhjv->nbhiv', w_intra, v)
    den_intra = jnp.sum(w_intra, -1)
    e_max = i_max[..., -1]
    k_end = k * jnp.exp(d_log[..., -1, :] - e_max[..., None])[..., None]

    def step(carry, inp):
        cm, nv, m = carry
        q_i, v_i, k_i, F_i, imax_i, numi, deni, emax_i = inp
        m_t = jnp.maximum(m[..., None] + F_i, imax_i)
        inter = jnp.exp(m[..., None] + F_i - m_t)
        intra = jnp.exp(imax_i - m_t)
        num = inter[..., None] * jnp.einsum('bhck,bhkv->bhcv', q_i, cm) + intra[..., None] * numi
        den = inter * jnp.einsum('bhck,bhk->bhc', q_i, nv) + intra * deni
        h = num / jnp.maximum(jnp.abs(den), jnp.exp(-m_t))[..., None]
        m_new = m_t[..., -1]
        a = jnp.exp(m + F_i[..., -1] - m_new)
        b = jnp.exp(emax_i - m_new)
        cm = a[..., None, None] * cm + b[..., None, None] * jnp.einsum('bhck,bhcv->bhkv', k_i, v_i)
        nv = a[..., None] * nv + b[..., None] * jnp.sum(k_i, -2)
        return (cm, nv, m_new), h

    (c_end, n_end, m_end), h = lax.scan(step, (c0, n0, m0),
                                        (q, v, k_end, F, i_max, num_intra, den_intra, e_max))
    return _from_chunks(h), c_end, n_end, m_end


def _token_mixers(x, conv_buf, s_gdn, c_m, n_m, m_m, w_in, conv_w, a_log, dt_bias, gdn_norm_w,
                  b_i, b_f, m_norm_w, w_br_a, w_br_b, w_out):
    B, L, _ = x.shape
    f32 = jnp.float32
    proj = x @ w_in
    cuts = [int(s) for s in np.cumsum(IN_SPLITS)[:-1]]
    qkv, z, b_raw, a_raw, qm, km, vm, om, im, fm, ga, gb = jnp.split(proj, cuts, axis=-1)
    xpad = jnp.concatenate([conv_buf.astype(qkv.dtype), qkv], 1)
    conv = xpad[:, 0:L] * conv_w[0]
    for w in range(1, CONV_W):
        conv = conv + xpad[:, w:w + L] * conv_w[w]
    new_buf = xpad[:, xpad.shape[1] - (CONV_W - 1):]
    conv = jax.nn.silu(conv.astype(f32))
    qa, ka, va = jnp.split(conv, [GDN_QK, 2 * GDN_QK], axis=-1)
    qa = _l2norm(qa.reshape(B, L, H_A, DK_A)) * (DK_A ** -0.5)
    ka = _l2norm(ka.reshape(B, L, H_A, DK_A))
    va = va.reshape(B, L, H_A, DV_A)
    beta = jax.nn.sigmoid(b_raw.astype(f32))
    g = -jnp.exp(a_log.astype(f32)) * jax.nn.softplus(a_raw.astype(f32) + dt_bias.astype(f32))
    oa, s_new = _gated_delta_rule(qa, ka, va, g, beta, s_gdn.astype(f32))
    oa = _rms_norm(oa, gdn_norm_w.astype(f32)) * jax.nn.silu(z.astype(f32).reshape(B, L, H_A, DV_A))
    qb = qm.astype(f32).reshape(B, L, H_B, DQK_B)
    kb = km.astype(f32).reshape(B, L, H_B, DQK_B) * (DQK_B ** -0.5)
    vb = vm.astype(f32).reshape(B, L, H_B, DV_B)
    ig = im.astype(f32) + b_i.astype(f32)
    lf = jax.nn.log_sigmoid(fm.astype(f32) + b_f.astype(f32))
    hb, c_new, n_new, m_new = _mlstm(qb, kb, vb, ig, lf, c_m.astype(f32), n_m.astype(f32), m_m.astype(f32))
    ob = jax.nn.sigmoid(om.astype(f32)).reshape(B, L, H_B, DV_B) * _rms_norm(hb, m_norm_w.astype(f32))
    ya = oa.reshape(B, L, GDN_V).astype(x.dtype) @ w_br_a
    yb = ob.reshape(B, L, M_V).astype(x.dtype) @ w_br_b
    mixed = jax.nn.sigmoid(ga) * ya + jax.nn.sigmoid(gb) * yb
    return mixed @ w_out, (new_buf, s_new, c_new, n_new, m_new)


def _peer(x, wq, keys, u_tab, v_tab):
    B, L, D = x.shape
    blk = PEER_BLOCK if L % PEER_BLOCK == 0 else L
    xb = x.reshape(-1, blk, D)

    def one_block(xt):
        q = (xt @ wq).reshape(blk, PEER_HEADS, 2, PEER_DKEY // 2)
        s = jnp.einsum('thpd,hpnd->thpn', q, keys).astype(jnp.float32)
        sv, si = lax.top_k(s, PEER_TOPK)
        cand = (sv[..., 0, :, None] + sv[..., 1, None, :]).reshape(blk, PEER_HEADS, PEER_TOPK * PEER_TOPK)
        cidx = (si[..., 0, :, None] * N_KEYS + si[..., 1, None, :]).reshape(blk, PEER_HEADS, PEER_TOPK * PEER_TOPK)
        top_v, top_p = lax.top_k(cand, PEER_TOPK)
        e_idx = jnp.take_along_axis(cidx, top_p, -1).reshape(blk, PEER_HEADS * PEER_TOPK)
        gate = jax.nn.softmax(top_v, -1).reshape(blk, PEER_HEADS * PEER_TOPK)
        act = jax.nn.gelu(jnp.einsum('ted,td->te', u_tab[e_idx], xt).astype(jnp.float32))
        coef = (gate * act).astype(xt.dtype)
        return jnp.einsum('te,ted->td', coef, v_tab[e_idx])

    return lax.map(one_block, xb).reshape(B, L, D)


def _layer(x, p_emb, conv_buf, s_gdn, c_m, n_m, m_m, w_in, conv_w, a_log, dt_bias, gdn_norm_w, b_i, b_f,
           m_norm_w, w_br_a, w_br_b, w_out, ln1_g, ln1_b, peer_wq, peer_keys, peer_u, peer_v,
           ln2_g, ln2_b, ple_proj, ple_gate):
    mix, states = _token_mixers(x, conv_buf, s_gdn, c_m, n_m, m_m, w_in, conv_w, a_log, dt_bias, gdn_norm_w,
                                b_i, b_f, m_norm_w, w_br_a, w_br_b, w_out)
    x = _layer_norm(ALPHA * x + mix, ln1_g, ln1_b)
    x = _layer_norm(ALPHA * x + _peer(x, peer_wq, peer_keys, peer_u, peer_v), ln2_g, ln2_b)
    x = x + jax.nn.sigmoid(x @ ple_gate) * (p_emb @ ple_proj)
    return x, states


def setup_inputs(seed: int = 0) -> dict:
    key = jax.random.key(seed)
    ks = iter(jax.random.split(key, 48))
    f32 = jnp.float32

    def nrm(shape, scale):
        return jax.random.normal(next(ks), shape, f32) * scale

    def unif(shape, lo, hi):
        return jax.random.uniform(next(ks), shape, f32, lo, hi)

    dt = jnp.exp(unif((DEPTH, H_A), float(np.log(1e-3)), float(np.log(1e-1))))
    return {
        "x_prompt": nrm((BATCH, SEQ, D_MODEL), 1.0),
        "x_sample": nrm((DEC_BATCH, DEC_SEQ, D_MODEL), 1.0),
        "state_gdn_conv": nrm((DEPTH, DEC_BATCH, CONV_W - 1, GDN_CONV_DIM), 1.0),
        "state_gdn_s": nrm((DEPTH, DEC_BATCH, H_A, DK_A, DV_A), 0.1),
        "state_mlstm_c": nrm((DEPTH, DEC_BATCH, H_B, DQK_B, DV_B), 1.0),
        "state_mlstm_n": nrm((DEPTH, DEC_BATCH, H_B, DQK_B), 1.0),
        "state_mlstm_m": nrm((DEPTH, DEC_BATCH, H_B), 1.0),
        "p_prompt": nrm((DEPTH, BATCH, SEQ, PLE_DIM), 1.0),
        "p_sample": nrm((DEPTH, DEC_BATCH, DEC_SEQ, PLE_DIM), 1.0),
        "ln0_g": 1.0 + nrm((D_MODEL,), 0.02),
        "ln0_b": nrm((D_MODEL,), 0.02),
        "w_in": nrm((DEPTH, D_MODEL, N_IN), D_MODEL ** -0.5),
        "gdn_conv_w": nrm((DEPTH, CONV_W, GDN_CONV_DIM), CONV_W ** -0.5),
        "gdn_a_log": jnp.log(unif((DEPTH, H_A), 1.0, 16.0)),
        "gdn_dt_bias": dt + jnp.log(-jnp.expm1(-dt)),
        "gdn_norm_w": 1.0 + nrm((DEPTH, DV_A), 0.02),
        "mlstm_b_i": nrm((DEPTH, H_B), 0.1),
        "mlstm_b_f": unif((DEPTH, H_B), 3.0, 6.0),
        "mlstm_norm_w": 1.0 + nrm((DEPTH, DV_B), 0.02),
        "w_branch_a": nrm((DEPTH, GDN_V, D_MODEL), BETA * GDN_V ** -0.5),
        "w_branch_b": nrm((DEPTH, M_V, D_MODEL), BETA * M_V ** -0.5),
        "w_out": nrm((DEPTH, D_MODEL, D_MODEL), BETA * D_MODEL ** -0.5),
        "ln1_g": 1.0 + nrm((DEPTH, D_MODEL), 0.02),
        "ln1_b": nrm((DEPTH, D_MODEL), 0.02),
        "peer_wq": nrm((DEPTH, D_MODEL, PEER_HEADS * PEER_DKEY), D_MODEL ** -0.5),
        "peer_keys": nrm((DEPTH, PEER_HEADS, 2, N_KEYS, PEER_DKEY // 2), (PEER_DKEY // 2) ** -0.5),
        "peer_u": nrm((DEPTH, N_EXPERTS, D_MODEL), D_MODEL ** -0.5),
        "peer_v": nrm((DEPTH, N_EXPERTS, D_MODEL), BETA * PEER_HEADS ** -0.5),
        "ln2_g": 1.0 + nrm((DEPTH, D_MODEL), 0.02),
        "ln2_b": nrm((DEPTH, D_MODEL), 0.02),
        "ple_proj": nrm((DEPTH, PLE_DIM, D_MODEL), PLE_DIM ** -0.5),
        "ple_gate": nrm((DEPTH, D_MODEL, D_MODEL), D_MODEL ** -0.5),
    }


def reference(x_prompt, x_sample, state_gdn_conv, state_gdn_s, state_mlstm_c, state_mlstm_n, state_mlstm_m,
              p_prompt, p_sample, ln0_g, ln0_b, w_in, gdn_conv_w, gdn_a_log, gdn_dt_bias, gdn_norm_w,
              mlstm_b_i, mlstm_b_f, mlstm_norm_w, w_branch_a, w_branch_b, w_out, ln1_g, ln1_b,
              peer_wq, peer_keys, peer_u, peer_v, ln2_g, ln2_b, ple_proj, ple_gate):
    def run(x, p, conv0, s0, c0, n0, m0):
        h = _layer_norm(x, ln0_g, ln0_b)
        new = ([], [], [], [], [])
        for i in range(DEPTH):
            h, st = _layer(h, p[i], conv0[i], s0[i], c0[i], n0[i], m0[i],
                           w_in[i], gdn_conv_w[i], gdn_a_log[i], gdn_dt_bias[i], gdn_norm_w[i],
                           mlstm_b_i[i], mlstm_b_f[i], mlstm_norm_w[i], w_branch_a[i], w_branch_b[i], w_out[i],
                           ln1_g[i], ln1_b[i], peer_wq[i], peer_keys[i], peer_u[i], peer_v[i],
                           ln2_g[i], ln2_b[i], ple_proj[i], ple_gate[i])
            for lst, s in zip(new, st):
                lst.append(s.astype(x.dtype))
        return h, [jnp.stack(l) for l in new]

    bp = x_prompt.shape[0]
    dtp = x_prompt.dtype
    y_prompt, (pc, ps, pC, pn, pm) = run(
        x_prompt, p_prompt,
        jnp.zeros((DEPTH, bp, CONV_W - 1, GDN_CONV_DIM), dtp),
        jnp.zeros((DEPTH, bp, H_A, DK_A, DV_A), dtp),
        jnp.zeros((DEPTH, bp, H_B, DQK_B, DV_B), dtp),
        jnp.zeros((DEPTH, bp, H_B, DQK_B), dtp),
        jnp.zeros((DEPTH, bp, H_B), dtp))
    y_sample, (sc, ss, sC, sn, sm) = run(
        x_sample, p_sample, state_gdn_conv, state_gdn_s, state_mlstm_c, state_mlstm_n, state_mlstm_m)
    return (y_prompt, y_sample, pc, ps, pC, pn, pm, sc, ss, sC, sn, sm)
```

```python
import functools

import jax
import jax.numpy as jnp
from jax import lax
from jax.experimental import pallas as pl
from jax.experimental.pallas import tpu as pltpu

F32 = jnp.float32
BF16 = jnp.bfloat16

D = 2048
H_A, DK_A, DV_A = 16, 128, 128
CONV_W = 4
GDN_QK = H_A * DK_A
GDN_V = H_A * DV_A
H_B, DQK_B, DV_B = 8, 128, 256
M_QK = H_B * DQK_B
M_V = H_B * DV_B
PEER_HEADS = 8
N_KEYS = 128
PEER_TOPK = 16
PLE_DIM = 256
DEPTH = 1
ALPHA = (2 * DEPTH) ** 0.25
LN_EPS = 1e-5
RMS_EPS = 1e-6
L2_EPS = 1e-6
CHUNK = 64

_REF_CUTS = (0, 6144, 8192, 8208, 8224, 9248, 10272, 12320, 14368, 14376, 14384, 16432, 18480)
Q0, K0, V0, Z0 = 0, 2048, 4096, 6144
MQ0, MK0, MV0, MO0 = 8192, 9216, 10240, 12288
GA0, GB0 = 14336, 16384
N_BIG = 18432
LANE_BETA, LANE_A, LANE_I, LANE_F = 0, 16, 32, 40
GATE_W = 128

LANES = 128
SUBLANES = 8
VMEM_LIMIT = 56 * 1024 * 1024

GDN_HB = 2
MLSTM_HB = 2
NEG_BIG = -1e30


def _bf(x):
    return x.astype(BF16)


def _dot(a, b):
    return jnp.dot(a, b, preferred_element_type=F32)


def _dot_nt(a, b):
    return lax.dot_general(a, b, (((1,), (1,)), ((), ())), preferred_element_type=F32)


def _dot_tn(a, b):
    return lax.dot_general(a, b, (((0,), (0,)), ((), ())), preferred_element_type=F32)


def _split2(x):
    hi = _bf(x)
    lo = _bf(x - hi.astype(F32))
    return hi, lo


def _split3(x):
    hi = _bf(x)
    r = x - hi.astype(F32)
    mid = _bf(r)
    lo = _bf(r - mid.astype(F32))
    return hi, mid, lo


def _dot3(a, b):
    ah, al = _split2(a)
    bh, bl = _split2(b)
    return (_dot(al, bh) + _dot(ah, bl)) + _dot(ah, bh)


def _dot_sel(sel_bf, x):
    hi, mid, lo = _split3(x)
    return (_dot(sel_bf, lo) + _dot(sel_bf, mid)) + _dot(sel_bf, hi)


def _ln(x, g, b):
    mu = jnp.mean(x, -1, keepdims=True)
    xc = x - mu
    var = jnp.mean(xc * xc, -1, keepdims=True)
    return xc * lax.rsqrt(var + LN_EPS) * g + b


def _sigmoid(x):
    return jax.nn.sigmoid(x)


def _softplus(x):
    return jnp.maximum(x, 0.0) + jnp.log1p(jnp.exp(-jnp.abs(x)))


def _log_sigmoid(x):
    return jnp.minimum(x, 0.0) - jnp.log1p(jnp.exp(-jnp.abs(x)))


def _gelu_tanh(x):
    c = 0.7978845608028654
    return 0.5 * x * (1.0 + jnp.tanh(c * (x + 0.044715 * (x * x * x))))


def _tri_masks(c):
    r = lax.broadcasted_iota(jnp.int32, (c, c), 0)
    col = lax.broadcasted_iota(jnp.int32, (c, c), 1)
    return r >= col, r > col


def _params(*sem):
    return pltpu.CompilerParams(dimension_semantics=sem, vmem_limit_bytes=VMEM_LIMIT)


def _row_tile(n, want):
    t = min(n, want)
    assert n % t == 0 and t % SUBLANES == 0, (n, t)
    return t


def _proj_kernel(x_ref, g_ref, b_ref, w_ref, wsh_ref, wsl_ref, o_ref, gates_ref, h_scr, *, tm, rb):
    @pl.when(pl.program_id(1) == 0)
    def _():
        def body(i, carry):
            r = pl.multiple_of(i * rb, rb)
            h = _ln(x_ref[pl.ds(r, rb), :], g_ref[...], b_ref[...])
            hh, hl = _split2(h)
            h_scr[pl.ds(r, rb), :] = hh
            gates_ref[pl.ds(r, rb), :] = (_dot(hl, wsh_ref[...]) + _dot(hh, wsl_ref[...])) + _dot(hh, wsh_ref[...])
            return carry

        lax.fori_loop(0, tm // rb, body, 0)

    o_ref[...] = _dot(h_scr[...], w_ref[...]).astype(o_ref.dtype)


def _proj(x, g0, b0, w_big, ws_hi, ws_lo):
    n = x.shape[0]
    tm = _row_tile(n, 1024)
    tn = 512
    rb = min(tm, 128)
    return pl.pallas_call(
        functools.partial(_proj_kernel, tm=tm, rb=rb),
        grid=(n // tm, N_BIG // tn),
        in_specs=[
            pl.BlockSpec((tm, D), lambda m, j: (m, 0)),
            pl.BlockSpec((1, D), lambda m, j: (0, 0)),
            pl.BlockSpec((1, D), lambda m, j: (0, 0)),
            pl.BlockSpec((D, tn), lambda m, j: (0, j)),
            pl.BlockSpec((D, GATE_W), lambda m, j: (0, 0)),
            pl.BlockSpec((D, GATE_W), lambda m, j: (0, 0)),
        ],
        out_specs=[
            pl.BlockSpec((tm, tn), lambda m, j: (m, j)),
            pl.BlockSpec((tm, GATE_W), lambda m, j: (m, 0)),
        ],
        out_shape=[
            jax.ShapeDtypeStruct((n, N_BIG), F32),
            jax.ShapeDtypeStruct((n, GATE_W), F32),
        ],
        scratch_shapes=[pltpu.VMEM((tm, D), BF16)],
        compiler_params=_params("parallel", "arbitrary"),
        name="proj",
    )(x, g0, b0, w_big, ws_hi, ws_lo)


def _chunk_cumsum(x, tri_bf, t):
    parts = [_dot_sel(tri_bf, x[c0:c0 + CHUNK]) for c0 in range(0, t, CHUNK)]
    return parts[0] if len(parts) == 1 else jnp.concatenate(parts, axis=0)


def _transpose_rows(x, t):
    if t % LANES:
        x = jnp.concatenate([x, jnp.zeros((LANES - t % LANES, LANES), F32)], axis=0)
    blocks = [x[r0:r0 + LANES].T for r0 in range(0, x.shape[0], LANES)]
    return blocks[0] if len(blocks) == 1 else jnp.concatenate(blocks, axis=1)


def _valid_rows(t_idx, t, lv):
    rows = lax.broadcasted_iota(jnp.int32, (t, GATE_W), 0) + t_idx * t
    return rows < lv


def _gdn_kernel(q_ref, k_ref, v_ref, z_ref, cwq_ref, cwk_ref, cwv_ref, cbq_ref, cbk_ref, cbv_ref,
                gt_ref, par_ref, nw_ref, s0_ref,
                oa_ref, s_out_ref, cq_out_ref, ck_out_ref, cv_out_ref,
                ext_scr, s_scr, *, t, hb, lv, seq):
    ti = pl.program_id(2)
    nt = seq // t
    t_state = (lv - 1) // t
    r_state = lv - t_state * t
    w = hb * DK_A

    @pl.when(ti == 0)
    def _():
        s_scr[...] = s0_ref[0]
        for i, cb in enumerate((cbq_ref, cbk_ref, cbv_ref)):
            ext_scr[i, pl.ds(SUBLANES - (CONV_W - 1), CONV_W - 1), :] = cb[0]

    conv = []
    for i, (x_ref, cw_ref, c_out) in enumerate(((q_ref, cwq_ref, cq_out_ref), (k_ref, cwk_ref, ck_out_ref),
                                                (v_ref, cwv_ref, cv_out_ref))):
        ext_scr[i, pl.ds(SUBLANES, t), :] = x_ref[...].astype(F32)
        base = SUBLANES - (CONV_W - 1)
        acc = ext_scr[i, pl.ds(base, t), :] * cw_ref[0:1, :]
        for j in range(1, CONV_W):
            acc = acc + ext_scr[i, pl.ds(base + j, t), :] * cw_ref[j:j + 1, :]
        conv.append(acc * _sigmoid(acc))

        @pl.when(ti == t_state)
        def _():
            c_out[0] = ext_scr[i, pl.ds(base + r_state, CONV_W - 1), :]

        ext_scr[i, pl.ds(0, SUBLANES), :] = ext_scr[i, pl.ds(t, SUBLANES), :]
    qc, kc, vc = conv

    gt = gt_ref[0]
    beta_f = _sigmoid(gt)
    g_f = -jnp.exp(par_ref[0, 0:1, :]) * _softplus(gt + par_ref[0, 1:2, :])
    if lv < seq:
        valid = _valid_rows(ti, t, lv)
        beta_f = jnp.where(valid, beta_f, 0.0)
        g_f = jnp.where(valid, g_f, 0.0)
    incl, strict = _tri_masks(CHUNK)
    tri_bf = jnp.where(incl, 1.0, 0.0).astype(BF16)
    gcum = _chunk_cumsum(g_f, tri_bf, t)
    gcum_t = _transpose_rows(gcum, t)

    zf = z_ref[...].astype(F32)
    nw = nw_ref[...]
    for hh in range(hb):
        sl = slice(hh * DK_A, (hh + 1) * DK_A)
        state = s_scr[hh]
        for c0 in range(0, t, CHUNK):
            rs = slice(c0, c0 + CHUNK)
            q = qc[rs, sl]
            k = kc[rs, sl]
            v = vc[rs, sl]
            q = q * lax.rsqrt(jnp.sum(q * q, -1, keepdims=True) + L2_EPS) * (DK_A ** -0.5)
            k = k * lax.rsqrt(jnp.sum(k * k, -1, keepdims=True) + L2_EPS)
            beta = beta_f[rs, LANE_BETA + hh:LANE_BETA + hh + 1]
            gcol = gcum[rs, LANE_A + hh:LANE_A + hh + 1]
            grow = gcum_t[LANE_A + hh:LANE_A + hh + 1, c0:c0 + CHUNK]
            glast = gcol[CHUNK - 1:CHUNK, :]
            decay = jnp.where(incl, jnp.exp(jnp.where(incl, gcol - grow, 0.0)), 0.0)
            kb = k * beta
            k_bf = _bf(k)
            kk = _dot_nt(_bf(kb), k_bf)
            m = jnp.where(strict, -(kk * decay), 0.0)
            ninv = m
            p = m
            for _ in range(5):
                p = _dot3(p, p)
                ninv = ninv + p + _dot3(ninv, p)
            eg = jnp.exp(gcol)
            rhs_v = v * beta
            rhs_k = kb * eg
            u_pre = rhs_v + _dot3(ninv, rhs_v)
            wmat = rhs_k + _dot3(ninv, rhs_k)
            attn = _dot_nt(_bf(q), k_bf) * decay
            q_g = q * eg
            k_dec = k * jnp.exp(glast - gcol)
            s_bf = _bf(state)
            u = u_pre - _dot(_bf(wmat), s_bf)
            u_bf = _bf(u)
            o = _dot(_bf(q_g), s_bf) + _dot(_bf(attn), u_bf)
            state = state * jnp.exp(glast) + _dot_tn(_bf(k_dec), u_bf)
            on = o * lax.rsqrt(jnp.mean(o * o, -1, keepdims=True) + RMS_EPS) * nw
            zz = zf[rs, sl]
            oa_ref[rs, sl] = (on * (zz * _sigmoid(zz))).astype(oa_ref.dtype)
        s_scr[hh] = state

    @pl.when(ti == nt - 1)
    def _():
        s_out_ref[0] = s_scr[...]


def _gdn(proj, gates_g, par_g, conv_w, norm_w, conv0, s0, *, bsz, seq, lv):
    t = min(seq, 256)
    hb = GDN_HB
    w = hb * DK_A
    nt = seq // t
    ng = H_A // hb

    def colspec(off):
        return pl.BlockSpec((t, w), lambda b, g, i: (b * nt + i, off // w + g))

    def cwspec(off):
        return pl.BlockSpec((CONV_W, w), lambda b, g, i: (0, off // w + g))

    def cbspec(off):
        return pl.BlockSpec((1, CONV_W - 1, w), lambda b, g, i: (b, 0, off // w + g))

    cstate = pl.BlockSpec((1, CONV_W - 1, w), lambda b, g, i: (b, 0, g))
    return pl.pallas_call(
        functools.partial(_gdn_kernel, t=t, hb=hb, lv=lv, seq=seq),
        grid=(bsz, ng, nt),
        in_specs=[
            colspec(Q0), colspec(K0), colspec(V0), colspec(Z0),
            cwspec(Q0), cwspec(K0), cwspec(V0),
            cbspec(Q0), cbspec(K0), cbspec(V0),
            pl.BlockSpec((1, t, GATE_W), lambda b, g, i: (g, b * nt + i, 0)),
            pl.BlockSpec((1, SUBLANES, GATE_W), lambda b, g, i: (g, 0, 0)),
            pl.BlockSpec((1, DV_A), lambda b, g, i: (0, 0)),
            pl.BlockSpec((1, hb, DK_A, DV_A), lambda b, g, i: (b, g, 0, 0)),
        ],
        out_specs=[
            pl.BlockSpec((t, w), lambda b, g, i: (b * nt + i, g)),
            pl.BlockSpec((1, hb, DK_A, DV_A), lambda b, g, i: (b, g, 0, 0)),
            cstate, cstate, cstate,
        ],
        out_shape=[
            jax.ShapeDtypeStruct((bsz * seq, GDN_V), BF16),
            jax.ShapeDtypeStruct((bsz, H_A, DK_A, DV_A), F32),
            jax.ShapeDtypeStruct((bsz, CONV_W - 1, GDN_QK), F32),
            jax.ShapeDtypeStruct((bsz, CONV_W - 1, GDN_QK), F32),
            jax.ShapeDtypeStruct((bsz, CONV_W - 1, GDN_V), F32),
        ],
        scratch_shapes=[
            pltpu.VMEM((3, t + SUBLANES, w), F32),
            pltpu.VMEM((hb, DK_A, DV_A), F32),
        ],
        compiler_params=_params("parallel", "parallel", "arbitrary"),
        name="gdn",
    )(proj, proj, proj, proj, conv_w, conv_w, conv_w, conv0, conv0, conv0, gates_g, par_g, norm_w, s0)


def _mlstm_kernel(q_ref, k_ref, v_ref, og_ref, gt_ref, par_ref, nw_ref, c0_ref, n0_ref, m0_ref,
                  ob_ref, c_out_ref, n_out_ref, m_out_ref,
                  c_scr, n_scr, m_scr, *, t, hb, lv, seq):
    ti = pl.program_id(2)
    nt = seq // t

    @pl.when(ti == 0)
    def _():
        c_scr[...] = c0_ref[0]
        n_scr[...] = n0_ref[0]
        m_scr[...] = m0_ref[0]

    gt = gt_ref[0]
    ig_f = gt + par_ref[0, 0:1, :]
    lf_f = _log_sigmoid(gt + par_ref[0, 1:2, :])
    if lv < seq:
        valid = _valid_rows(ti, t, lv)
        ig_f = jnp.where(valid, ig_f, NEG_BIG)
        lf_f = jnp.where(valid, lf_f, 0.0)
    incl, _ = _tri_masks(CHUNK)
    tri_bf = jnp.where(incl, 1.0, 0.0).astype(BF16)
    fcum = _chunk_cumsum(lf_f, tri_bf, t)
    fcum_t = _transpose_rows(fcum, t)
    ig_t = _transpose_rows(ig_f, t)

    nw = nw_ref[...]
    for hh in range(hb):
        qs = slice(hh * DQK_B, (hh + 1) * DQK_B)
        vs = slice(hh * DV_B, (hh + 1) * DV_B)
        cm = c_scr[hh]
        nv = n_scr[hh]
        mm = m_scr[hh][:, 0:1]
        for c0 in range(0, t, CHUNK):
            rs = slice(c0, c0 + CHUNK)
            q = q_ref[rs, qs].astype(F32)
            k = k_ref[rs, qs].astype(F32) * (DQK_B ** -0.5)
            v_bf = _bf(v_ref[rs, vs])
            q_bf = _bf(q)
            fcol = fcum[rs, LANE_F + hh:LANE_F + hh + 1]
            frow = fcum_t[LANE_F + hh:LANE_F + hh + 1, c0:c0 + CHUNK]
            igcol = ig_f[rs, LANE_I + hh:LANE_I + hh + 1]
            igrow = ig_t[LANE_I + hh:LANE_I + hh + 1, c0:c0 + CHUNK]
            flast = fcol[CHUNK - 1:CHUNK, :]
            dlog = jnp.where(incl, fcol - frow + igrow, -jnp.inf)
            imax = jnp.max(dlog, -1, keepdims=True)
            w_in = jnp.exp(dlog - imax) * _dot_nt(q_bf, _bf(k))
            num_in = _dot(_bf(w_in), v_bf)
            den_in = jnp.sum(w_in, -1, keepdims=True)
            emax = imax[CHUNK - 1:CHUNK, :]
            k_end = k * jnp.exp(flast - fcol + igcol - emax)
            mf = mm + fcol
            m_t = jnp.maximum(mf, imax)
            inter = jnp.exp(mf - m_t)
            intra = jnp.exp(imax - m_t)
            num = inter * _dot(q_bf, _bf(cm)) + intra * num_in
            den = inter * jnp.sum(q * nv, -1, keepdims=True) + intra * den_in
            h = num / jnp.maximum(jnp.abs(den), jnp.exp(-m_t))
            m_new = m_t[CHUNK - 1:CHUNK, :]
            a = jnp.exp(mm + flast - m_new)
            b = jnp.exp(emax - m_new)
            cm = a * cm + b * _dot_tn(_bf(k_end), v_bf)
            nv = a * nv + b * jnp.sum(k_end, 0, keepdims=True)
            mm = m_new
            hn = h * lax.rsqrt(jnp.mean(h * h, -1, keepdims=True) + RMS_EPS) * nw
            ob_ref[rs, vs] = (_sigmoid(og_ref[rs, vs].astype(F32)) * hn).astype(ob_ref.dtype)
        c_scr[hh] = cm
        n_scr[hh] = nv
        m_scr[hh] = jnp.broadcast_to(mm, (1, LANES))

    @pl.when(ti == nt - 1)
    def _():
        c_out_ref[0] = c_scr[...]
        n_out_ref[0] = n_scr[...]
        m_out_ref[0] = m_scr[...]


def _mlstm(proj, gates_g, par_g, norm_w, c0, n0, m0, *, bsz, seq, lv):
    t = min(seq, 256)
    hb = MLSTM_HB
    nt = seq // t
    ng = H_B // hb
    wq = hb * DQK_B
    wv = hb * DV_B
    cspec = pl.BlockSpec((1, hb, DQK_B, DV_B), lambda b, g, i: (b, g, 0, 0))
    nspec = pl.BlockSpec((1, hb, 1, DQK_B), lambda b, g, i: (b, g, 0, 0))
    return pl.pallas_call(
        functools.partial(_mlstm_kernel, t=t, hb=hb, lv=lv, seq=seq),
        grid=(bsz, ng, nt),
        in_specs=[
            pl.BlockSpec((t, wq), lambda b, g, i: (b * nt + i, MQ0 // wq + g)),
            pl.BlockSpec((t, wq), lambda b, g, i: (b * nt + i, MK0 // wq + g)),
            pl.BlockSpec((t, wv), lambda b, g, i: (b * nt + i, MV0 // wv + g)),
            pl.BlockSpec((t, wv), lambda b, g, i: (b * nt + i, MO0 // wv + g)),
            pl.BlockSpec((1, t, GATE_W), lambda b, g, i: (g, b * nt + i, 0)),
            pl.BlockSpec((1, SUBLANES, GATE_W), lambda b, g, i: (g, 0, 0)),
            pl.BlockSpec((1, DV_B), lambda b, g, i: (0, 0)),
            cspec, nspec, nspec,
        ],
        out_specs=[
            pl.BlockSpec((t, wv), lambda b, g, i: (b * nt + i, g)),
            cspec, nspec, nspec,
        ],
        out_shape=[
            jax.ShapeDtypeStruct((bsz * seq, M_V), BF16),
            jax.ShapeDtypeStruct((bsz, H_B, DQK_B, DV_B), F32),
            jax.ShapeDtypeStruct((bsz, H_B, 1, DQK_B), F32),
            jax.ShapeDtypeStruct((bsz, H_B, 1, LANES), F32),
        ],
        scratch_shapes=[
            pltpu.VMEM((hb, DQK_B, DV_B), F32),
            pltpu.VMEM((hb, 1, DQK_B), F32),
            pltpu.VMEM((hb, 1, LANES), F32),
        ],
        compiler_params=_params("parallel", "parallel", "arbitrary"),
        name="mlstm",
    )(proj, proj, proj, proj, gates_g, par_g, norm_w, c0, n0, m0)


def _mix_kernel(oa_ref, ob_ref, wa_ref, wb_ref, ga_ref, gb_ref, o_ref):
    ya = _dot(oa_ref[...], wa_ref[...])
    yb = _dot(ob_ref[...], wb_ref[...])
    o_ref[...] = (_sigmoid(ga_ref[...].astype(F32)) * ya + _sigmoid(gb_ref[...].astype(F32)) * yb).astype(o_ref.dtype)


def _mix(oa, ob, wa, wb, proj):
    n = oa.shape[0]
    tm = _row_tile(n, 1024)
    tn = 512
    return pl.pallas_call(
        _mix_kernel,
        grid=(n // tm, D // tn),
        in_specs=[
            pl.BlockSpec((tm, GDN_V), lambda m, j: (m, 0)),
            pl.BlockSpec((tm, M_V), lambda m, j: (m, 0)),
            pl.BlockSpec((GDN_V, tn), lambda m, j: (0, j)),
            pl.BlockSpec((M_V, tn), lambda m, j: (0, j)),
            pl.BlockSpec((tm, tn), lambda m, j: (m, GA0 // tn + j)),
            pl.BlockSpec((tm, tn), lambda m, j: (m, GB0 // tn + j)),
        ],
        out_specs=pl.BlockSpec((tm, tn), lambda m, j: (m, j)),
        out_shape=jax.ShapeDtypeStruct((n, D), BF16),
        compiler_params=_params("parallel", "arbitrary"),
        name="mix",
    )(oa, ob, wa, wb, proj, proj)


def _outln_kernel(x_ref, g0_ref, b0_ref, mixed_ref, w_ref, g1_ref, b1_ref, o_ref, *, tm, rb):
    def body(i, carry):
        r = pl.multiple_of(i * rb, rb)
        h = _ln(x_ref[pl.ds(r, rb), :], g0_ref[...], b0_ref[...])
        y = _dot(mixed_ref[pl.ds(r, rb), :], w_ref[...])
        o_ref[pl.ds(r, rb), :] = _ln(ALPHA * h + y, g1_ref[...], b1_ref[...])
        return carry

    lax.fori_loop(0, tm // rb, body, 0)


def _outln(x, g0, b0, mixed, w_out, g1, b1):
    n = x.shape[0]
    tm = _row_tile(n, 512)
    rb = min(tm, 256)
    vec = pl.BlockSpec((1, D), lambda m: (0, 0))
    return pl.pallas_call(
        functools.partial(_outln_kernel, tm=tm, rb=rb),
        grid=(n // tm,),
        in_specs=[
            pl.BlockSpec((tm, D), lambda m: (m, 0)), vec, vec,
            pl.BlockSpec((tm, D), lambda m: (m, 0)),
            pl.BlockSpec((D, D), lambda m: (0, 0)),
            vec, vec,
        ],
        out_specs=pl.BlockSpec((tm, D), lambda m: (m, 0)),
        out_shape=jax.ShapeDtypeStruct((n, D), F32),
        compiler_params=_params("parallel"),
        name="outln",
    )(x, g0, b0, mixed, w_out, g1, b1)


def _topk_rows(s, kk):
    n, tm = s.shape
    rows = lax.broadcasted_iota(jnp.int32, (n, tm), 0).astype(F32)
    slot = lax.broadcasted_iota(jnp.int32, (kk, tm), 0)
    work = s
    rank = jnp.full((n, tm), float(kk), F32)
    vals = jnp.zeros((kk, tm), F32)
    for r in range(kk):
        mx = jnp.max(work, 0, keepdims=True)
        idx = jnp.min(jnp.where(work == mx, rows, float(n)), 0, keepdims=True)
        sel = rows == idx
        rank = jnp.where(sel, float(r), rank)
        work = jnp.where(sel, -jnp.inf, work)
        vals = jnp.where(slot == r, mx, vals)
    return vals, rank


def _peerq_kernel(x_ref, wq_ref, keys_ref, at_ref, bt_ref, r1_ref, len_ref, xb_scr, *, tm, rb):
    @pl.when(pl.program_id(1) == 0)
    def _():
        def body(i, carry):
            r = pl.multiple_of(i * rb, rb)
            xb_scr[pl.ds(r, rb), :] = _bf(x_ref[pl.ds(r, rb), :])
            return carry

        lax.fori_loop(0, tm // rb, body, 0)

    qp = _bf(_dot(xb_scr[...], wq_ref[...]))
    s0 = _dot_nt(keys_ref[0, 0], qp[:, :N_KEYS])
    s1 = _dot_nt(keys_ref[0, 1], qp[:, N_KEYS:])
    sv0, rank0 = _topk_rows(s0, PEER_TOPK)
    sv1, rank1 = _topk_rows(s1, PEER_TOPK)
    cand = jnp.concatenate([sv0[a:a + 1, :] + sv1 for a in range(PEER_TOPK)], axis=0)
    ncand = PEER_TOPK * PEER_TOPK
    pos = lax.broadcasted_iota(jnp.int32, (ncand, tm), 0).astype(F32)
    ra_iota = lax.broadcasted_iota(jnp.int32, (PEER_TOPK, tm), 0).astype(F32)
    lens = jnp.zeros((PEER_TOPK, tm), F32)
    zsum = jnp.zeros((1, tm), F32)
    top0 = None
    for r in range(PEER_TOPK):
        mx = jnp.max(cand, 0, keepdims=True)
        idx = jnp.min(jnp.where(cand == mx, pos, float(ncand)), 0, keepdims=True)
        cand = jnp.where(pos == idx, -jnp.inf, cand)
        if top0 is None:
            top0 = mx
        zsum = zsum + jnp.exp(mx - top0)
        ra = jnp.floor(idx * (1.0 / PEER_TOPK))
        lens = lens + jnp.where(ra_iota == ra, 1.0, 0.0)
    len0 = jnp.zeros((N_KEYS, tm), F32)
    for a in range(PEER_TOPK):
        len0 = len0 + jnp.where(rank0 == float(a), lens[a:a + 1, :], 0.0)
    at_ref[0] = jnp.exp(s0 - sv0[0:1, :]) / zsum
    bt_ref[0] = jnp.exp(s1 - sv1[0:1, :])
    r1_ref[0] = rank1
    len_ref[0] = len0


def _peerq(x1, wq, keys):
    n = x1.shape[0]
    tm = _row_tile(n, 256)
    rb = min(tm, 128)
    dk = 2 * N_KEYS
    mspec = pl.BlockSpec((1, N_KEYS, tm), lambda m, h: (h, 0, m))
    mshape = jax.ShapeDtypeStruct((PEER_HEADS, N_KEYS, n), F32)
    return pl.pallas_call(
        functools.partial(_peerq_kernel, tm=tm, rb=rb),
        grid=(n // tm, PEER_HEADS),
        in_specs=[
            pl.BlockSpec((tm, D), lambda m, h: (m, 0)),
            pl.BlockSpec((D, dk), lambda m, h: (0, h)),
            pl.BlockSpec((1, 2, N_KEYS, N_KEYS), lambda m, h: (h, 0, 0, 0)),
        ],
        out_specs=[mspec, mspec, mspec, mspec],
        out_shape=[mshape, mshape, mshape, mshape],
        scratch_shapes=[pltpu.VMEM((tm, D), BF16)],
        compiler_params=_params("parallel", "arbitrary"),
        name="peerq",
    )(x1, wq, keys)


def _peerx_kernel(x_ref, at_ref, bt_ref, r1_ref, len_ref, u_ref, v_ref, g_ref, b_ref, o_ref,
                  xb_scr, acc_scr, *, tm, rb, ib):
    step = pl.program_id(1)
    nsteps = pl.num_programs(1)

    @pl.when(step == 0)
    def _():
        def body(i, carry):
            r = pl.multiple_of(i * rb, rb)
            xb_scr[pl.ds(r, rb), :] = _bf(x_ref[pl.ds(r, rb), :])
            return carry

        lax.fori_loop(0, tm // rb, body, 0)
        acc_scr[...] = jnp.zeros_like(acc_scr)

    act = _gelu_tanh(_dot_nt(u_ref[...], xb_scr[...]))
    parts = []
    for j in range(ib):
        i0 = step * ib + j
        coef = jnp.zeros((N_KEYS, tm), F32)
        for h in range(PEER_HEADS):
            lenrow = len_ref[h, pl.ds(i0, 1), :]
            arow = at_ref[h, pl.ds(i0, 1), :]
            coef = coef + jnp.where(r1_ref[h] < lenrow, bt_ref[h], 0.0) * arow
        parts.append(coef)
    coef = parts[0] if ib == 1 else jnp.concatenate(parts, axis=0)
    acc_scr[...] += _dot_tn(_bf(coef * act), v_ref[...])

    @pl.when(step == nsteps - 1)
    def _():
        def body(i, carry):
            r = pl.multiple_of(i * rb, rb)
            o_ref[pl.ds(r, rb), :] = _ln(ALPHA * x_ref[pl.ds(r, rb), :] + acc_scr[pl.ds(r, rb), :],
                                         g_ref[...], b_ref[...])
            return carry

        lax.fori_loop(0, tm // rb, body, 0)


def _peerx(x1, at, bt, r1, ln0, u_bf, v_bf, g2, b2):
    n = x1.shape[0]
    tm = _row_tile(n, 512)
    rb = min(tm, 128)
    ib = 2
    mspec = pl.BlockSpec((PEER_HEADS, N_KEYS, tm), lambda m, s: (0, 0, m))
    vec = pl.BlockSpec((1, D), lambda m, s: (0, 0))
    return pl.pallas_call(
        functools.partial(_peerx_kernel, tm=tm, rb=rb, ib=ib),
        grid=(n // tm, N_KEYS // ib),
        in_specs=[
            pl.BlockSpec((tm, D), lambda m, s: (m, 0)),
            mspec, mspec, mspec, mspec,
            pl.BlockSpec((ib * N_KEYS, D), lambda m, s: (s, 0)),
            pl.BlockSpec((ib * N_KEYS, D), lambda m, s: (s, 0)),
            vec, vec,
        ],
        out_specs=pl.BlockSpec((tm, D), lambda m, s: (m, 0)),
        out_shape=jax.ShapeDtypeStruct((n, D), F32),
        scratch_shapes=[pltpu.VMEM((tm, D), BF16), pltpu.VMEM((tm, D), F32)],
        compiler_params=_params("parallel", "arbitrary"),
        name="peerx",
    )(x1, at, bt, r1, ln0, u_bf, v_bf, g2, b2)


def _ple_kernel(x_ref, p_ref, wg_ref, wp_ref, o_ref, xb_scr, *, tm, tn, rb):
    j = pl.program_id(1)

    @pl.when(j == 0)
    def _():
        def body(i, carry):
            r = pl.multiple_of(i * rb, rb)
            xb_scr[pl.ds(r, rb), :] = _bf(x_ref[pl.ds(r, rb), :])
            return carry

        lax.fori_loop(0, tm // rb, body, 0)

    gate = _sigmoid(_dot(xb_scr[...], wg_ref[...]))
    emb = _dot(_bf(p_ref[...]), wp_ref[...])
    col = pl.multiple_of(j * tn, tn)
    o_ref[...] = x_ref[:, pl.ds(col, tn)] + gate * emb


def _ple(x2, p, wg, wp):
    n = x2.shape[0]
    tm = _row_tile(n, 1024)
    tn = 512
    rb = min(tm, 128)
    return pl.pallas_call(
        functools.partial(_ple_kernel, tm=tm, tn=tn, rb=rb),
        grid=(n // tm, D // tn),
        in_specs=[
            pl.BlockSpec((tm, D), lambda m, j: (m, 0)),
            pl.BlockSpec((tm, PLE_DIM), lambda m, j: (m, 0)),
            pl.BlockSpec((D, tn), lambda m, j: (0, j)),
            pl.BlockSpec((PLE_DIM, tn), lambda m, j: (0, j)),
        ],
        out_specs=pl.BlockSpec((tm, tn), lambda m, j: (m, j)),
        out_shape=jax.ShapeDtypeStruct((n, D), F32),
        scratch_shapes=[pltpu.VMEM((tm, D), BF16)],
        compiler_params=_params("parallel", "arbitrary"),
        name="ple",
    )(x2, p, wg, wp)


def _row(v):
    return v.reshape(1, -1).astype(F32)


def _lane_table(entries):
    tab = jnp.zeros((SUBLANES, GATE_W), F32)
    for r, off, vec in entries:
        tab = tab.at[r, off:off + vec.shape[0]].set(vec.astype(F32))
    return tab


def _group_rolls(a, hb, ng):
    return jnp.stack([jnp.roll(a, -g * hb, axis=-1) for g in range(ng)])


def _prepare_weights(w_in, conv_w, a_log, dt_bias, b_i, b_f, w_br_a, w_br_b, w_out, peer_wq, peer_keys,
                     peer_u, peer_v, ple_proj, ple_gate):
    c = _REF_CUTS
    w_big = jnp.concatenate([w_in[:, c[0]:c[2]], w_in[:, c[4]:c[8]], w_in[:, c[10]:c[12]]], axis=1).astype(BF16)
    ws = jnp.concatenate([w_in[:, c[2]:c[4]], w_in[:, c[8]:c[10]],
                          jnp.zeros((D, GATE_W - 2 * H_A - 2 * H_B), F32)], axis=1)
    ws_hi = ws.astype(BF16)
    ws_lo = (ws - ws_hi.astype(F32)).astype(BF16)
    gdn_par = _group_rolls(_lane_table([(0, LANE_A, a_log), (1, LANE_A, dt_bias)]), GDN_HB, H_A // GDN_HB)
    ml_par = _group_rolls(_lane_table([(0, LANE_I, b_i), (1, LANE_F, b_f)]), MLSTM_HB, H_B // MLSTM_HB)
    return dict(
        w_big=w_big, ws_hi=ws_hi, ws_lo=ws_lo, gdn_par=gdn_par, ml_par=ml_par, conv_w=conv_w.astype(F32),
        wa=w_br_a.astype(BF16), wb=w_br_b.astype(BF16), w_out=w_out.astype(BF16),
        wq=peer_wq.astype(BF16), keys=peer_keys.astype(BF16), u=peer_u.astype(BF16), v=peer_v.astype(BF16),
        wp=ple_proj.astype(BF16), wg=ple_gate.astype(BF16),
    )


def _run(x, p, conv0, s0, c0, n0, m0, wts, vecs, *, lv):
    bsz, seq, _ = x.shape
    n = bsz * seq
    xf = x.reshape(n, D)
    proj, gates = _proj(xf, vecs["ln0_g"], vecs["ln0_b"], wts["w_big"], wts["ws_hi"], wts["ws_lo"])
    gates_a = _group_rolls(gates, GDN_HB, H_A // GDN_HB)
    gates_b = _group_rolls(gates, MLSTM_HB, H_B // MLSTM_HB)
    oa, s_new, cq, ck, cv = _gdn(proj, gates_a, wts["gdn_par"], wts["conv_w"], vecs["gdn_norm_w"], conv0, s0,
                                 bsz=bsz, seq=seq, lv=lv)
    ob, c_new, n_new, m_new = _mlstm(proj, gates_b, wts["ml_par"], vecs["mlstm_norm_w"],
                                     c0, n0.reshape(bsz, H_B, 1, DQK_B),
                                     jnp.broadcast_to(m0.reshape(bsz, H_B, 1, 1), (bsz, H_B, 1, LANES)),
                                     bsz=bsz, seq=seq, lv=lv)
    mixed = _mix(oa, ob, wts["wa"], wts["wb"], proj)
    x1 = _outln(xf, vecs["ln0_g"], vecs["ln0_b"], mixed, wts["w_out"], vecs["ln1_g"], vecs["ln1_b"])
    at, bt, r1, ln0 = _peerq(x1, wts["wq"], wts["keys"])
    x2 = _peerx(x1, at, bt, r1, ln0, wts["u"], wts["v"], vecs["ln2_g"], vecs["ln2_b"])
    y = _ple(x2, p.reshape(n, PLE_DIM), wts["wg"], wts["wp"])
    conv_new = jnp.concatenate([cq, ck, cv], axis=-1)
    return (y.reshape(bsz, seq, D), conv_new[None], s_new[None], c_new[None],
            n_new.reshape(1, bsz, H_B, DQK_B), m_new[:, :, 0, 0][None])


def kernel(x_prompt, x_sample, state_gdn_conv, state_gdn_s, state_mlstm_c, state_mlstm_n, state_mlstm_m,
           p_prompt, p_sample, ln0_g, ln0_b, w_in, gdn_conv_w, gdn_a_log, gdn_dt_bias, gdn_norm_w,
           mlstm_b_i, mlstm_b_f, mlstm_norm_w, w_branch_a, w_branch_b, w_out, ln1_g, ln1_b,
           peer_wq, peer_keys, peer_u, peer_v, ln2_g, ln2_b, ple_proj, ple_gate):
    assert w_in.shape[0] == DEPTH == 1
    wts = _prepare_weights(w_in[0], gdn_conv_w[0], gdn_a_log[0], gdn_dt_bias[0], mlstm_b_i[0], mlstm_b_f[0],
                           w_branch_a[0], w_branch_b[0], w_out[0], peer_wq[0], peer_keys[0], peer_u[0], peer_v[0],
                           ple_proj[0], ple_gate[0])
    vecs = dict(ln0_g=_row(ln0_g), ln0_b=_row(ln0_b), ln1_g=_row(ln1_g[0]), ln1_b=_row(ln1_b[0]),
                ln2_g=_row(ln2_g[0]), ln2_b=_row(ln2_b[0]), gdn_norm_w=_row(gdn_norm_w[0]),
                mlstm_norm_w=_row(mlstm_norm_w[0]))

    def padded(a, seq_pad):
        return jnp.pad(a, ((0, 0), (0, seq_pad - a.shape[1]), (0, 0)))

    def run(x, p, conv0, s0, c0, n0, m0):
        bsz, seq, _ = x.shape
        seq_pad = -(-seq // CHUNK) * CHUNK
        outs = _run(padded(x, seq_pad), padded(p, seq_pad), conv0, s0, c0, n0, m0, wts, vecs, lv=seq)
        return (outs[0][:, :seq],) + outs[1:]

    bp = x_prompt.shape[0]
    prompt = run(x_prompt, p_prompt[0],
                 jnp.zeros((bp, CONV_W - 1, 2 * GDN_QK + GDN_V), F32),
                 jnp.zeros((bp, H_A, DK_A, DV_A), F32),
                 jnp.zeros((bp, H_B, DQK_B, DV_B), F32),
                 jnp.zeros((bp, H_B, DQK_B), F32),
                 jnp.zeros((bp, H_B), F32))
    sample = run(x_sample, p_sample[0], state_gdn_conv[0], state_gdn_s[0], state_mlstm_c[0],
                 state_mlstm_n[0], state_mlstm_m[0])
    return (prompt[0], sample[0]) + prompt[1:] + sample[1:]
```
